```python
import math
import jax, jax.numpy as jnp
from jax import lax
import numpy as np

D_MODEL = 2048
BATCH = 4
SEQ = 2048
DEPTH = 4
DEC_BATCH = 8
DEC_SEQ = 1
PAST_LEN = 16384
PAGE_SIZE = 128

N_MIXERS = 3
N_MAMBA = (DEPTH + 2) // 3
N_FOX = (DEPTH + 1) // 3
N_NSA = DEPTH // 3

DN_ALPHA = (2 * DEPTH) ** 0.25
DN_BETA = (8 * DEPTH) ** -0.25
LN_EPS = 1e-5
NEG = -1e30

M_D_INNER = 2 * D_MODEL
M_HEADDIM = 64
M_HEADS = M_D_INNER // M_HEADDIM
M_GROUPS = 8
M_STATE = 128
M_CONV = 4
M_CHUNK = 128
M_CONV_CH = M_D_INNER + 2 * M_GROUPS * M_STATE
M_IN = M_D_INNER + M_CONV_CH + M_HEADS

N_HEADS = 16
HEAD_DIM = D_MODEL // N_HEADS
N_KV = 4
REP = N_HEADS // N_KV
ATT_SCALE = HEAD_DIM ** -0.5
Q_BLOCK = 128
F_IN = N_HEADS * HEAD_DIM + 2 * N_KV * HEAD_DIM + N_HEADS

CMP_STRIDE = 16
CMP_LEN = 2 * CMP_STRIDE
SEL_BLOCK = 64
SEL_TOPK = 16
WINDOW = 512
SEL_QCHUNK = 16
FORCE_BONUS = 1e6
N_BRANCH_KV = 6
N_IN = N_HEADS * HEAD_DIM + N_BRANCH_KV * N_KV * HEAD_DIM + 3 * N_HEADS

MEM_LEN = 256
MEM_HEADS = 4
MEM_HD = D_MODEL // MEM_HEADS
MEM_SCALE = MEM_HD ** -0.5

P_HEADS = 8
P_NKEYS = 128
P_EXPERTS = P_NKEYS * P_NKEYS
P_QDIM = 256
P_TOPK = 16
P_TOKBLOCK = 128

kernel_name = 'hybrid_ssd_fox_nsa_peer_decode_step'


def layer_norm(x, g, b):
    xf = x.astype(jnp.float32)
    mu = jnp.mean(xf, axis=-1, keepdims=True)
    var = jnp.mean(jnp.square(xf - mu), axis=-1, keepdims=True)
    return ((xf - mu) * lax.rsqrt(var + LN_EPS) * g.astype(jnp.float32) + b.astype(jnp.float32)).astype(x.dtype)


def gather_pages(pool, page_table):
    rows = pool[page_table]
    return rows.reshape(page_table.shape[0], page_table.shape[1] * pool.shape[1], *pool.shape[2:])


def causal_dwconv(u, w, bias):
    c = u.shape[-1]
    out = lax.conv_general_dilated(u, w[:, None, :].astype(u.dtype), window_strides=(1,), padding='VALID',
                                   dimension_numbers=('NWC', 'WIO', 'NWC'), feature_group_count=c)
    return out + bias.astype(u.dtype)


def ssd_scan(x, dt, a_head, bm, cm, s0):
    b, L, H, P = x.shape
    G, N = bm.shape[2], bm.shape[3]
    R = H // G
    Q = M_CHUNK if L % M_CHUNK == 0 else L
    nc = L // Q
    la = (dt * a_head).reshape(b, nc, Q, G, R)
    xdt = (x * dt[..., None]).reshape(b, nc, Q, G, R, P)
    bc = bm.reshape(b, nc, Q, G, N)
    cc = cm.reshape(b, nc, Q, G, N)
    acum = jnp.cumsum(la, axis=2)
    causal = jnp.tril(jnp.ones((Q, Q), dtype=bool))
    seg = acum[:, :, :, None] - acum[:, :, None, :]
    decay = jnp.exp(jnp.where(causal[:, :, None, None], seg, -jnp.inf))
    cb = jnp.einsum('bctgn,bcsgn->bctsg', cc, bc)
    y_diag = jnp.einsum('bctsg,bctsgr,bcsgrp->bctgrp', cb, decay, xdt)
    to_end = jnp.exp(acum[:, :, -1:] - acum)
    chunk_states = jnp.einsum('bcsgn,bcsgr,bcsgrp->bcgrpn', bc, to_end, xdt)
    chunk_decay = jnp.exp(acum[:, :, -1])

    def step(s, inp):
        st, dec = inp
        return s * dec[..., None, None] + st, s

    s_fin, s_in = lax.scan(step, s0.reshape(b, G, R, P, N),
                           (jnp.moveaxis(chunk_states, 1, 0), jnp.moveaxis(chunk_decay, 1, 0)))
    s_in = jnp.moveaxis(s_in, 0, 1)
    y_off = jnp.einsum('bctgn,bcgrpn,bctgr->bctgrp', cc, s_in, jnp.exp(acum))
    y = (y_diag + y_off).reshape(b, L, H, P)
    return y, s_fin.reshape(b, H, P, N)


def grouped_rms_norm(y, g):
    b, L, dn = y.shape
    yf = y.astype(jnp.float32).reshape(b, L, M_GROUPS, dn // M_GROUPS)
    yf = yf * lax.rsqrt(jnp.mean(jnp.square(yf), axis=-1, keepdims=True) + LN_EPS)
    return yf.reshape(b, L, dn) * g.astype(jnp.float32)


def mamba_mixer(h, conv_buf, ssm0, w_in, conv_w, conv_b, dt_bias, a_log, d_skip, norm_g, w_out):
    b, L, _ = h.shape
    zxbcdt = h @ w_in
    z = zxbcdt[..., :M_D_INNER]
    xbc = zxbcdt[..., M_D_INNER:M_D_INNER + M_CONV_CH]
    dt = zxbcdt[..., M_D_INNER + M_CONV_CH:]
    u = jnp.concatenate([conv_buf.astype(xbc.dtype), xbc], axis=1)
    new_buf = u[:, u.shape[1] - (M_CONV - 1):]
    xbc = jax.nn.silu(causal_dwconv(u, conv_w, conv_b))
    nbc = M_GROUPS * M_STATE
    xs = xbc[..., :M_D_INNER].reshape(b, L, M_HEADS, M_HEADDIM).astype(jnp.float32)
    bm = xbc[..., M_D_INNER:M_D_INNER + nbc].reshape(b, L, M_GROUPS, M_STATE).astype(jnp.float32)
    cm = xbc[..., M_D_INNER + nbc:].reshape(b, L, M_GROUPS, M_STATE).astype(jnp.float32)
    dt = jax.nn.softplus(dt.astype(jnp.float32) + dt_bias.astype(jnp.float32))
    a_head = -jnp.exp(a_log.astype(jnp.float32))
    y, s_fin = ssd_scan(xs, dt, a_head, bm, cm, ssm0.astype(jnp.float32))
    y = y + d_skip.astype(jnp.float32)[:, None] * xs
    y = y.reshape(b, L, M_D_INNER) * jax.nn.silu(z.astype(jnp.float32))
    y = grouped_rms_norm(y, norm_g).astype(h.dtype)
    return y @ w_out, new_buf, s_fin.astype(h.dtype)


def fox_attend(q, cq, k, v, ck, q_offset):
    b, T = q.shape[0], q.shape[1]
    S = k.shape[1]
    qb = Q_BLOCK if T % Q_BLOCK == 0 else T
    nb = T // qb
    kpos = jnp.arange(S)
    ckh = ck.reshape(b, S, N_KV, REP).transpose(0, 2, 3, 1)
    qs = jnp.moveaxis(q.reshape(b, nb, qb, N_KV, REP, HEAD_DIM), 1, 0)
    cqs = jnp.moveaxis(cq.reshape(b, nb, qb, N_KV, REP), 1, 0)

    def block(args):
        i, qi, ci = args
        s = jnp.einsum('btgrd,bsgd->bgrts', qi, k).astype(jnp.float32) * ATT_SCALE
        s = s + ci.transpose(0, 2, 3, 1)[..., None] - ckh[:, :, :, None, :]
        qpos = q_offset + i * qb + jnp.arange(qb)
        s = jnp.where(kpos[None, :] <= qpos[:, None], s, NEG)
        p = jax.nn.softmax(s, axis=-1).astype(v.dtype)
        return jnp.einsum('bgrts,bsgd->btgrd', p, v)

    o = lax.map(block, (jnp.arange(nb), qs, cqs))
    return jnp.moveaxis(o, 0, 1).reshape(b, T, N_HEADS * HEAD_DIM)


def fox_mixer(h, past_kv, past_logf, w_in, b_f, w_out):
    b, T, _ = h.shape
    P = past_kv.shape[1]
    nq, nkv = N_HEADS * HEAD_DIM, N_KV * HEAD_DIM
    proj = h @ w_in
    q = proj[..., :nq].reshape(b, T, N_KV, REP, HEAD_DIM)
    kv = proj[..., nq:nq + 2 * nkv].reshape(b, T, 2, N_KV, HEAD_DIM)
    logf = jax.nn.log_sigmoid(proj[..., nq + 2 * nkv:].astype(jnp.float32) + b_f.astype(jnp.float32))
    kv_all = jnp.concatenate([past_kv.astype(kv.dtype), kv], axis=1)
    c_all = jnp.cumsum(jnp.concatenate([past_logf.astype(jnp.float32), logf], axis=1), axis=1)
    o = fox_attend(q, c_all[:, P:], kv_all[:, :, 0], kv_all[:, :, 1], c_all, P)
    return o.astype(h.dtype) @ w_out, kv, logf.astype(h.dtype)


def nsa_compress(k, wpos, proj):
    b, lp, g, d = k.shape
    kr = k.reshape(b, lp // CMP_STRIDE, CMP_STRIDE, g, d)
    head = jnp.einsum('bnigd,ig->bngd', kr, wpos[:CMP_STRIDE].astype(k.dtype))
    tail = jnp.einsum('bnigd,ig->bngd', kr, wpos[CMP_STRIDE:].astype(k.dtype))
    pooled = head[:, :-1] + tail[:, 1:]
    return jnp.einsum('bngd,gde->bnge', pooled, proj.astype(k.dtype))


def nsa_select(q, qpos, ksb, vsb, top_idx, top_ok):
    b, T = q.shape[0], q.shape[1]
    n = top_idx.shape[-1]
    qc = SEL_QCHUNK if T % SEL_QCHUNK == 0 else T
    nch = T // qc
    bi = jnp.arange(b)[:, None, None]
    gi = jnp.arange(N_KV)[None, :, None]

    def chunk(args):
        qi, pi, ii, oki = args
        flat = ii.transpose(0, 2, 1, 3).reshape(b, N_KV, qc * n)
        kg = ksb[bi, gi, flat].reshape(b, N_KV, qc, n * SEL_BLOCK, HEAD_DIM)
        vg = vsb[bi, gi, flat].reshape(b, N_KV, qc, n * SEL_BLOCK, HEAD_DIM)
        kpos = (ii[..., None] * SEL_BLOCK + jnp.arange(SEL_BLOCK)).reshape(b, qc, N_KV, n * SEL_BLOCK)
        ok = (kpos <= pi[None, :, None, None]) & jnp.repeat(oki, SEL_BLOCK, axis=-1)
        s = jnp.einsum('bcgrd,bgcmd->bcgrm', qi, kg).astype(jnp.float32) * ATT_SCALE
        s = jnp.where(ok[:, :, :, None, :], s, NEG)
        p = jax.nn.softmax(s, axis=-1).astype(vg.dtype)
        return jnp.einsum('bcgrm,bgcmd->bcgrd', p, vg)

    o = lax.map(chunk, (jnp.moveaxis(q.reshape(b, nch, qc, N_KV, REP, HEAD_DIM), 1, 0),
                        qpos.reshape(nch, qc),
                        jnp.moveaxis(top_idx.reshape(b, nch, qc, N_KV, n), 1, 0),
                        jnp.moveaxis(top_ok.reshape(b, nch, qc, N_KV, n), 1, 0)))
    return jnp.moveaxis(o, 0, 1).reshape(b, T, N_KV, REP, HEAD_DIM)


def window_attend(q, qpos, wk, wv, k_offset):
    b, T = q.shape[0], q.shape[1]
    qb = Q_BLOCK if T % Q_BLOCK == 0 else T
    nb = T // qb
    idx = jnp.arange(nb)[:, None] * qb + jnp.arange(WINDOW + qb)[None, :]
    kband = wk[:, idx]
    vband = wv[:, idx]
    kpos = k_offset + idx
    qp = qpos.reshape(nb, qb)
    kp = kpos[:, None, :]
    ok = (kp <= qp[:, :, None]) & (kp > qp[:, :, None] - WINDOW) & (kp >= 0)
    qr = q.reshape(b, nb, qb, N_KV, REP, HEAD_DIM)
    s = jnp.einsum('bitgrd,bisgd->bigrts', qr, kband).astype(jnp.float32) * ATT_SCALE
    s = jnp.where(ok[None, :, None, None], s, NEG)
    p = jax.nn.softmax(s, axis=-1).astype(vband.dtype)
    o = jnp.einsum('bigrts,bisgd->bitgrd', p, vband)
    return o.reshape(b, T, N_KV, REP, HEAD_DIM)


def nsa_mixer(h, past_kv, win_buf, w_in, b_gate, cmp_wpos, cmp_proj, w_out):
    b, T, _ = h.shape
    P = past_kv.shape[1]
    nq, nkv = N_HEADS * HEAD_DIM, N_KV * HEAD_DIM
    proj = h @ w_in
    q = proj[..., :nq].reshape(b, T, N_KV, REP, HEAD_DIM)
    kv = proj[..., nq:nq + N_BRANCH_KV * nkv].reshape(b, T, N_BRANCH_KV, N_KV, HEAD_DIM)
    gates = jax.nn.sigmoid(proj[..., nq + N_BRANCH_KV * nkv:].astype(jnp.float32)
                           + b_gate.astype(jnp.float32)).reshape(b, T, N_KV, REP, 3)
    paged_new = kv[:, :, :4]
    L = P + T
    lp = -(-L // SEL_BLOCK) * SEL_BLOCK
    full = jnp.concatenate([past_kv.astype(kv.dtype), paged_new], axis=1)
    full = jnp.pad(full, ((0, 0), (0, lp - L), (0, 0), (0, 0), (0, 0)))
    qpos = P + jnp.arange(T)
    kc = nsa_compress(full[:, :, 0], cmp_wpos[0], cmp_proj[0])
    vc = nsa_compress(full[:, :, 1], cmp_wpos[1], cmp_proj[1])
    nc = kc.shape[1]
    cstart = jnp.arange(nc) * CMP_STRIDE
    cmask = (cstart + CMP_LEN - 1)[None, :] <= qpos[:, None]
    sc = jnp.einsum('btgrd,bngd->btgrn', q, kc).astype(jnp.float32) * ATT_SCALE
    sc = jnp.where(cmask[None, :, None, None, :], sc, NEG)
    pc = jax.nn.softmax(sc, axis=-1) * cmask[None, :, None, None, :]
    o_cmp = jnp.einsum('btgrn,bngd->btgrd', pc.astype(vc.dtype), vc)
    ns = lp // SEL_BLOCK
    sstart = jnp.arange(ns) * SEL_BLOCK
    overlap = ((cstart[:, None] < sstart[None, :] + SEL_BLOCK)
               & (cstart[:, None] + CMP_LEN > sstart[None, :])).astype(jnp.float32)
    imp = jnp.einsum('btgrn,nm->btgm', pc, overlap)
    qblk = qpos // SEL_BLOCK
    sidx = jnp.arange(ns)
    forced = (sidx[None, :] == 0) | (sidx[None, :] == qblk[:, None]) | (sidx[None, :] == qblk[:, None] - 1)
    valid = sidx[None, :] <= qblk[:, None]
    imp = jnp.where(forced[None, :, None, :], imp + FORCE_BONUS, imp)
    imp = jnp.where(valid[None, :, None, :], imp, NEG)
    top_val, top_idx = lax.top_k(imp, min(SEL_TOPK, ns))
    top_ok = top_val > 0.5 * NEG
    ksb = full[:, :, 2].reshape(b, ns, SEL_BLOCK, N_KV, HEAD_DIM).transpose(0, 3, 1, 2, 4)
    vsb = full[:, :, 3].reshape(b, ns, SEL_BLOCK, N_KV, HEAD_DIM).transpose(0, 3, 1, 2, 4)
    o_slc = nsa_select(q, qpos, ksb, vsb, top_idx, top_ok)
    wk = jnp.concatenate([win_buf[:, :, 0].astype(kv.dtype), kv[:, :, 4]], axis=1)
    wv = jnp.concatenate([win_buf[:, :, 1].astype(kv.dtype), kv[:, :, 5]], axis=1)
    o_win = window_attend(q, qpos, wk, wv, P - WINDOW)
    new_win = jnp.stack([wk, wv], axis=2)[:, T:]
    o = gates[..., 0:1] * o_cmp + gates[..., 1:2] * o_slc + gates[..., 2:3] * o_win
    o = o.reshape(b, T, N_HEADS * HEAD_DIM).astype(h.dtype)
    return o @ w_out, paged_new, new_win


def memory_attend(h, mem_kv, wq, wo):
    b, L, _ = h.shape
    q = (h @ wq).reshape(b, L, MEM_HEADS, MEM_HD)
    s = jnp.einsum('blhd,bmhd->bhlm', q, mem_kv[:, :, 0].astype(q.dtype)).astype(jnp.float32) * MEM_SCALE
    p = jax.nn.softmax(s, axis=-1).astype(h.dtype)
    o = jnp.einsum('bhlm,bmhd->blhd', p, mem_kv[:, :, 1].astype(h.dtype)).reshape(b, L, D_MODEL)
    return o @ wo


def peer_ffn(h, wq, sub_keys, u_tab, v_tab):
    shp = h.shape
    t = h.reshape(-1, D_MODEL)
    T = t.shape[0]
    q = (t @ wq).reshape(T, P_HEADS, 2, P_QDIM // 2)
    s = jnp.einsum('thkd,hknd->thkn', q, sub_keys).astype(jnp.float32)
    v1, i1 = lax.top_k(s[:, :, 0], P_TOPK)
    v2, i2 = lax.top_k(s[:, :, 1], P_TOPK)
    cand = (v1[..., :, None] + v2[..., None, :]).reshape(T, P_HEADS, P_TOPK * P_TOPK)
    cidx = (i1[..., :, None] * P_NKEYS + i2[..., None, :]).reshape(T, P_HEADS, P_TOPK * P_TOPK)
    top_s, pos = lax.top_k(cand, P_TOPK)
    eidx = jnp.take_along_axis(cidx, pos, axis=-1)
    gate = jax.nn.softmax(top_s, axis=-1)
    blk = P_TOKBLOCK if T % P_TOKBLOCK == 0 else T
    nb = T // blk

    def block(args):
        tb, eb, gb = args
        act = jax.nn.gelu(jnp.einsum('td,thkd->thk', tb, u_tab[eb]).astype(jnp.float32), approximate=False)
        return jnp.einsum('thk,thkd->td', (gb * act).astype(v_tab.dtype), v_tab[eb])

    out = lax.map(block, (t.reshape(nb, blk, D_MODEL), eidx.reshape(nb, blk, P_HEADS, P_TOPK),
                          gate.reshape(nb, blk, P_HEADS, P_TOPK)))
    return out.reshape(shp).astype(h.dtype)


def setup_inputs(seed: int = 0) -> dict:
    key = jax.random.key(seed)
    keys = iter(jax.random.split(key, 64))

    def nrm(shape, scale):
        return jax.random.normal(next(keys), shape, jnp.float32) * scale

    def unif(shape, lo, hi):
        return jax.random.uniform(next(keys), shape, jnp.float32, lo, hi)

    n_pages = PAST_LEN // PAGE_SIZE
    n_used = DEC_BATCH * n_pages
    n_pool = n_used + max(1, n_used // 4)
    x_prompt = nrm((BATCH, SEQ, D_MODEL), 1.0)
    x_sample = nrm((DEC_BATCH, DEC_SEQ, D_MODEL), 1.0)
    cache_mem_kv = nrm((DEPTH, DEC_BATCH, MEM_LEN, 2, MEM_HEADS, MEM_HD), 1.0)
    state_ssm = nrm((N_MAMBA, DEC_BATCH, M_HEADS, M_HEADDIM, M_STATE), 0.5)
    state_conv = nrm((N_MAMBA, DEC_BATCH, M_CONV - 1, M_CONV_CH), 1.0)
    cache_fox_kv = nrm((N_FOX, n_pool, PAGE_SIZE, 2, N_KV, HEAD_DIM), 1.0)
    cache_fox_logf = jax.nn.log_sigmoid(2.0 + nrm((N_FOX, n_pool, PAGE_SIZE, N_HEADS), 0.5))
    cache_nsa_kv = nrm((N_NSA, n_pool, PAGE_SIZE, 4, N_KV, HEAD_DIM), 1.0)
    state_nsa_win = nrm((N_NSA, DEC_BATCH, WINDOW, 2, N_KV, HEAD_DIM), 1.0)
    page_table = jax.random.permutation(next(keys), n_pool)[:n_used].reshape(DEC_BATCH, n_pages).astype(jnp.int32)
    mem_prompt = nrm((BATCH, MEM_LEN, D_MODEL), 1.0)
    dt0 = jnp.exp(unif((N_MAMBA, M_HEADS), math.log(1e-3), math.log(1e-1)))
    sd = D_MODEL ** -0.5
    return {
        'x_prompt': x_prompt, 'x_sample': x_sample,
        'cache_mem_kv': cache_mem_kv, 'state_ssm': state_ssm, 'state_conv': state_conv,
        'cache_fox_kv': cache_fox_kv, 'cache_fox_logf': cache_fox_logf,
        'cache_nsa_kv': cache_nsa_kv, 'state_nsa_win': state_nsa_win,
        'page_table': page_table, 'mem_prompt': mem_prompt,
        'ln_g': 1.0 + nrm((DEPTH, 3, D_MODEL), 0.02),
        'ln_b': nrm((DEPTH, 3, D_MODEL), 0.02),
        'mem_wq': nrm((DEPTH, D_MODEL, D_MODEL), sd),
        'mem_wkv': nrm((DEPTH, D_MODEL, 2 * D_MODEL), sd),
        'mem_wo': nrm((DEPTH, D_MODEL, D_MODEL), sd * DN_BETA),
        'peer_wq': nrm((DEPTH, D_MODEL, P_HEADS * P_QDIM), sd),
        'peer_subkeys': nrm((DEPTH, P_HEADS, 2, P_NKEYS, P_QDIM // 2), (P_QDIM // 2) ** -0.5),
        'peer_u': nrm((DEPTH, P_EXPERTS, D_MODEL), sd),
        'peer_v': nrm((DEPTH, P_EXPERTS, D_MODEL), DN_BETA * P_HEADS ** -0.5),
        'mamba_w_in': nrm((N_MAMBA, D_MODEL, M_IN), sd),
        'mamba_conv_w': nrm((N_MAMBA, M_CONV, M_CONV_CH), M_CONV ** -0.5),
        'mamba_conv_b': nrm((N_MAMBA, M_CONV_CH), 0.02),
        'mamba_dt_bias': dt0 + jnp.log(-jnp.expm1(-dt0)),
        'mamba_a_log': jnp.log(unif((N_MAMBA, M_HEADS), 1.0, 16.0)),
        'mamba_d': 1.0 + nrm((N_MAMBA, M_HEADS), 0.02),
        'mamba_norm_g': 1.0 + nrm((N_MAMBA, M_D_INNER), 0.02),
        'mamba_w_out': nrm((N_MAMBA, M_D_INNER, D_MODEL), M_D_INNER ** -0.5 * DN_BETA),
        'fox_w_in': nrm((N_FOX, D_MODEL, F_IN), sd),
        'fox_b_f': 2.0 + nrm((N_FOX, N_HEADS), 0.1),
        'fox_w_out': nrm((N_FOX, D_MODEL, D_MODEL), sd * DN_BETA),
        'nsa_w_in': nrm((N_NSA, D_MODEL, N_IN), sd),
        'nsa_b_gate': nrm((N_NSA, 3 * N_HEADS), 0.02),
        'nsa_cmp_wpos': (1.0 + nrm((N_NSA, 2, CMP_LEN, N_KV), 0.1)) / CMP_LEN,
        'nsa_cmp_proj': nrm((N_NSA, 2, N_KV, HEAD_DIM, HEAD_DIM), HEAD_DIM ** -0.5),
        'nsa_w_out': nrm((N_NSA, D_MODEL, D_MODEL), sd * DN_BETA),
    }


def reference(x_prompt, x_sample, cache_mem_kv, state_ssm, state_conv, cache_fox_kv, cache_fox_logf,
              cache_nsa_kv, state_nsa_win, page_table, mem_prompt, ln_g, ln_b, mem_wq, mem_wkv, mem_wo,
              peer_wq, peer_subkeys, peer_u, peer_v, mamba_w_in, mamba_conv_w, mamba_conv_b, mamba_dt_bias,
              mamba_a_log, mamba_d, mamba_norm_g, mamba_w_out, fox_w_in, fox_b_f, fox_w_out,
              nsa_w_in, nsa_b_gate, nsa_cmp_wpos, nsa_cmp_proj, nsa_w_out):
    hp, hs = x_prompt, x_sample
    bp, bs = hp.shape[0], hs.shape[0]
    dtp = hp.dtype
    mem_kv_p, ssm_p, conv_p, fox_kv_p, fox_lf_p, nsa_kv_p, nsa_win_p = [], [], [], [], [], [], []
    ssm_s, conv_s, fox_kv_s, fox_lf_s, nsa_kv_s, nsa_win_s = [], [], [], [], [], []
    for i in range(DEPTH):
        kind, j = i % N_MIXERS, i // N_MIXERS
        if kind == 0:
            mw = (mamba_w_in[j], mamba_conv_w[j], mamba_conv_b[j], mamba_dt_bias[j], mamba_a_log[j],
                  mamba_d[j], mamba_norm_g[j], mamba_w_out[j])
            op, bufp, stp = mamba_mixer(hp, jnp.zeros((bp, M_CONV - 1, M_CONV_CH), dtp),
                                        jnp.zeros((bp, M_HEADS, M_HEADDIM, M_STATE), jnp.float32), *mw)
            os_, bufs, sts = mamba_mixer(hs, state_conv[j], state_ssm[j], *mw)
            conv_p.append(bufp); ssm_p.append(stp); conv_s.append(bufs); ssm_s.append(sts)
        elif kind == 1:
            past_kv = gather_pages(cache_fox_kv[j], page_table)
            past_lf = gather_pages(cache_fox_logf[j], page_table)
            op, kvp, lfp = fox_mixer(hp, jnp.zeros((bp, 0, 2, N_KV, HEAD_DIM), dtp),
                                     jnp.zeros((bp, 0, N_HEADS), dtp), fox_w_in[j], fox_b_f[j], fox_w_out[j])
            os_, kvs, lfs = fox_mixer(hs, past_kv, past_lf, fox_w_in[j], fox_b_f[j], fox_w_out[j])
            fox_kv_p.append(kvp); fox_lf_p.append(lfp); fox_kv_s.append(kvs); fox_lf_s.append(lfs)
        else:
            past = gather_pages(cache_nsa_kv[j], page_table)
            nw = (nsa_w_in[j], nsa_b_gate[j], nsa_cmp_wpos[j], nsa_cmp_proj[j], nsa_w_out[j])
            op, kvp, winp = nsa_mixer(hp, jnp.zeros((bp, 0, 4, N_KV, HEAD_DIM), dtp),
                                      jnp.zeros((bp, WINDOW, 2, N_KV, HEAD_DIM), dtp), *nw)
            os_, kvs, wins = nsa_mixer(hs, past, state_nsa_win[j], *nw)
            nsa_kv_p.append(kvp); nsa_win_p.append(winp); nsa_kv_s.append(kvs); nsa_win_s.append(wins)
        hp = layer_norm(DN_ALPHA * hp + op, ln_g[i, 0], ln_b[i, 0])
        hs = layer_norm(DN_ALPHA * hs + os_, ln_g[i, 0], ln_b[i, 0])
        mkv = (mem_prompt @ mem_wkv[i]).reshape(bp, MEM_LEN, 2, MEM_HEADS, MEM_HD)
        mem_kv_p.append(mkv)
        hp = layer_norm(DN_ALPHA * hp + memory_attend(hp, mkv, mem_wq[i], mem_wo[i]), ln_g[i, 1], ln_b[i, 1])
        hs = layer_norm(DN_ALPHA * hs + memory_attend(hs, cache_mem_kv[i], mem_wq[i], mem_wo[i]), ln_g[i, 1], ln_b[i, 1])
        hp = layer_norm(DN_ALPHA * hp + peer_ffn(hp, peer_wq[i], peer_subkeys[i], peer_u[i], peer_v[i]), ln_g[i, 2], ln_b[i, 2])
        hs = layer_norm(DN_ALPHA * hs + peer_ffn(hs, peer_wq[i], peer_subkeys[i], peer_u[i], peer_v[i]), ln_g[i, 2], ln_b[i, 2])
    return (hp, hs,
            jnp.stack(mem_kv_p), jnp.stack(ssm_p), jnp.stack(conv_p),
            jnp.stack(fox_kv_p), jnp.stack(fox_lf_p), jnp.stack(nsa_kv_p), jnp.stack(nsa_win_p),
            jnp.stack(ssm_s), jnp.stack(conv_s), jnp.stack(fox_kv_s), jnp.stack(fox_lf_s),
            jnp.stack(nsa_kv_s), jnp.stack(nsa_win_s))
```

```python
import functools
import math

import jax
import jax.numpy as jnp
from jax import lax
from jax.experimental import pallas as pl
from jax.experimental.pallas import tpu as pltpu

F32 = jnp.float32
BF16 = jnp.bfloat16
HIGHEST = lax.Precision.HIGHEST

D_MODEL = 2048
DEPTH = 4
DN_ALPHA = (2 * DEPTH) ** 0.25
LN_EPS = 1e-5
NEG = -1e30

LANES = 128
VMEM_LIMIT_BYTES = 56 * 1024 * 1024

M_D_INNER = 2 * D_MODEL
M_HEADDIM = 64
M_HEADS = M_D_INNER // M_HEADDIM
M_GROUPS = 8
M_STATE = 128
M_CONV = 4
M_CHUNK = 128
M_CONV_CH = M_D_INNER + 2 * M_GROUPS * M_STATE
M_IN = M_D_INNER + M_CONV_CH + M_HEADS

N_HEADS = 16
HEAD_DIM = D_MODEL // N_HEADS
N_KV = 4
REP = N_HEADS // N_KV
ATT_SCALE = HEAD_DIM ** -0.5

CMP_STRIDE = 16
CMP_LEN = 2 * CMP_STRIDE
SEL_BLOCK = 64
SEL_TOPK = 16
WINDOW = 512
FORCE_BONUS = 1e6
N_BRANCH_KV = 6

MEM_LEN = 256
MEM_HEADS = 4
MEM_HD = D_MODEL // MEM_HEADS
MEM_SCALE = MEM_HD ** -0.5

P_HEADS = 8
P_NKEYS = 128
P_EXPERTS = P_NKEYS * P_NKEYS
P_QDIM = 256
P_TOPK = 16

_NT = (((1,), (1,)), ((), ()))


def _params(*sem):
    return pltpu.CompilerParams(dimension_semantics=sem, vmem_limit_bytes=VMEM_LIMIT_BYTES)


def _dot(a, b, **kw):
    return jnp.dot(a, b, preferred_element_type=F32, **kw)


def _dot_nt(a, b, **kw):
    return lax.dot_general(a, b, _NT, preferred_element_type=F32, **kw)


def _layer_norm(v, g, b):
    mu = jnp.mean(v, axis=-1, keepdims=True)
    d = v - mu
    var = jnp.mean(d * d, axis=-1, keepdims=True)
    return d * lax.rsqrt(var + LN_EPS) * g + b


def _iota(shape, dim):
    return lax.broadcasted_iota(jnp.int32, shape, dim)


def _linear_kernel(x_ref, w_ref, o_ref, xb_ref):
    @pl.when(pl.program_id(1) == 0)
    def _():
        xb_ref[...] = x_ref[...].astype(BF16)

    o_ref[...] = _dot(xb_ref[...], w_ref[...])


def linear(x, w, tm=1024, tn=512):
    m, k = x.shape
    n = w.shape[1]
    tm = min(tm, m)
    return pl.pallas_call(
        _linear_kernel,
        grid=(pl.cdiv(m, tm), pl.cdiv(n, tn)),
        in_specs=[pl.BlockSpec((tm, k), lambda i, j: (i, 0)),
                  pl.BlockSpec((k, tn), lambda i, j: (0, j))],
        out_specs=pl.BlockSpec((tm, tn), lambda i, j: (i, j)),
        out_shape=jax.ShapeDtypeStruct((m, n), F32),
        scratch_shapes=[pltpu.VMEM((tm, k), BF16)],
        compiler_params=_params("arbitrary", "arbitrary"),
        name="linear",
    )(x, w)


def _linear_res_ln_kernel(y_ref, w_ref, x_ref, g_ref, b_ref, o_ref, acc_ref, *, nk):
    k = pl.program_id(1)

    @pl.when(k == 0)
    def _():
        acc_ref[...] = jnp.zeros_like(acc_ref)

    acc_ref[...] += _dot(y_ref[...].astype(BF16), w_ref[...])

    @pl.when(k == nk - 1)
    def _():
        o_ref[...] = _layer_norm(DN_ALPHA * x_ref[...] + acc_ref[...], g_ref[...], b_ref[...])


def linear_res_ln(y, w, x, g, b, tm=512, tk=512):
    m, k = y.shape
    d = w.shape[1]
    tm = min(tm, m)
    nk = k // tk
    return pl.pallas_call(
        functools.partial(_linear_res_ln_kernel, nk=nk),
        grid=(m // tm, nk),
        in_specs=[pl.BlockSpec((tm, tk), lambda i, kk: (i, kk)),
                  pl.BlockSpec((tk, d), lambda i, kk: (kk, 0)),
                  pl.BlockSpec((tm, d), lambda i, kk: (i, 0)),
                  pl.BlockSpec((1, d), lambda i, kk: (0, 0)),
                  pl.BlockSpec((1, d), lambda i, kk: (0, 0))],
        out_specs=pl.BlockSpec((tm, d), lambda i, kk: (i, 0)),
        out_shape=jax.ShapeDtypeStruct((m, d), F32),
        scratch_shapes=[pltpu.VMEM((tm, d), F32)],
        compiler_params=_params("arbitrary", "arbitrary"),
        name="linear_res_ln",
    )(y, w, x, g.reshape(1, d), b.reshape(1, d))


def _mem_attn_kernel(q_ref, kv_ref, o_ref):
    for h in range(MEM_HEADS):
        lo, hi = h * MEM_HD, (h + 1) * MEM_HD
        q = q_ref[0, :, lo:hi].astype(BF16)
        k = kv_ref[0, :, lo:hi].astype(BF16)
        v = kv_ref[0, :, D_MODEL + lo:D_MODEL + hi].astype(BF16)
        s = _dot_nt(q, k) * MEM_SCALE
        p = jnp.exp(s - jnp.max(s, axis=-1, keepdims=True))
        inv = 1.0 / jnp.sum(p, axis=-1, keepdims=True)
        o_ref[0, :, lo:hi] = _dot((p * inv).astype(BF16), v)


def mem_attention(q, mem_kv, tq=512):
    b, t, d = q.shape
    tq = min(tq, t)
    return pl.pallas_call(
        _mem_attn_kernel,
        grid=(b, t // tq),
        in_specs=[pl.BlockSpec((1, tq, d), lambda i, j: (i, j, 0)),
                  pl.BlockSpec((1, MEM_LEN, 2 * d), lambda i, j: (i, 0, 0))],
        out_specs=pl.BlockSpec((1, tq, d), lambda i, j: (i, j, 0)),
        out_shape=jax.ShapeDtypeStruct((b, t, d), F32),
        compiler_params=_params("arbitrary", "arbitrary"),
        name="mem_attention",
    )(q, mem_kv)


def mem_layer(h, mem_kv, wq, wo, g, b, batch):
    n, d = h.shape
    q = linear(h, wq).reshape(batch, n // batch, d)
    o = mem_attention(q, mem_kv).reshape(n, d)
    return linear_res_ln(o, wo, h, g, b)


_PEER_CANDS = tuple((a, c) for a in range(P_TOPK) for c in range(P_TOPK) if (a + 1) * (c + 1) <= P_TOPK)
_PEER_NCAND = -(-len(_PEER_CANDS) // 8) * 8


def _top_values(x, n, fill):
    tops = []
    cur = x
    for _ in range(n):
        m = jnp.max(cur, axis=0, keepdims=True)
        tops.append(m)
        cur = jnp.where(cur >= m, fill, cur)
    return tops


def _peer_router_kernel(q_ref, sk_ref, e1_ref, e2_ref, tau_ref, c_ref, cn_ref):
    tq = q_ref.shape[0]
    c_ref[...] = jnp.full(c_ref.shape, -1.0, F32)
    cn_ref[...] = jnp.full(cn_ref.shape, jnp.inf, F32)
    for h in range(P_HEADS):
        e_full, e_top = [], []
        for half in range(2):
            col = (h * 2 + half) * (P_QDIM // 2)
            qb = q_ref[:, col:col + P_QDIM // 2].astype(BF16)
            s = _dot_nt(sk_ref[h, half].astype(BF16), qb)
            tops = _top_values(s, P_TOPK, -jnp.inf)
            e_full.append(jnp.where(s >= tops[-1], jnp.exp(s - tops[0]), 0.0))
            e_top.append([jnp.exp(t - tops[0]) for t in tops])
        for r, (a, c) in enumerate(_PEER_CANDS):
            c_ref[r:r + 1, :] = e_top[0][a] * e_top[1][c]
        ctops = _top_values(c_ref[...], P_TOPK, -1.0)
        z = ctops[0]
        for m in ctops[1:]:
            z = z + jnp.maximum(m, 0.0)
        rz = 1.0 / z
        for r, (a, c) in enumerate(_PEER_CANDS):
            cn_ref[r:r + 1, :] = e_top[0][a] * (e_top[1][c] * rz)
        tau = jnp.min(jnp.where(c_ref[...] >= ctops[-1], cn_ref[...], jnp.inf), axis=0, keepdims=True)
        e1_ref[h] = e_full[0]
        e2_ref[h] = e_full[1] * rz
        tau_ref[h:h + 1, :] = tau


def peer_router(q, sub_keys, tq=256):
    t = q.shape[0]
    tq = min(tq, t)
    return pl.pallas_call(
        _peer_router_kernel,
        grid=(t // tq,),
        in_specs=[pl.BlockSpec((tq, P_HEADS * P_QDIM), lambda i: (i, 0)),
                  pl.BlockSpec((P_HEADS, 2, P_NKEYS, P_QDIM // 2), lambda i: (0, 0, 0, 0))],
        out_specs=[pl.BlockSpec((P_HEADS, P_NKEYS, tq), lambda i: (0, 0, i)),
                   pl.BlockSpec((P_HEADS, P_NKEYS, tq), lambda i: (0, 0, i)),
                   pl.BlockSpec((P_HEADS, tq), lambda i: (0, i))],
        out_shape=[jax.ShapeDtypeStruct((P_HEADS, P_NKEYS, t), F32),
                   jax.ShapeDtypeStruct((P_HEADS, P_NKEYS, t), F32),
                   jax.ShapeDtypeStruct((P_HEADS, t), F32)],
        scratch_shapes=[pltpu.VMEM((_PEER_NCAND, tq), F32), pltpu.VMEM((_PEER_NCAND, tq), F32)],
        compiler_params=_params("arbitrary"),
        name="peer_router",
    )(q, sub_keys)


def _gelu(x):
    return 0.5 * x * (1.0 + lax.erf(x * math.sqrt(0.5)))


def _peer_experts_kernel(t_ref, u_ref, vt_ref, e1_ref, e2_ref, tau_ref, g_ref, b_ref, o_ref,
                         tb_ref, acc_ref, wa_ref, *, ne, rows):
    e = pl.program_id(1)

    @pl.when(e == 0)
    def _():
        tb_ref[...] = t_ref[...].astype(BF16)
        acc_ref[...] = jnp.zeros_like(acc_ref)

    ht = _dot_nt(u_ref[...], tb_ref[...])
    for j in range(rows):
        act = _gelu(ht[j * P_NKEYS:(j + 1) * P_NKEYS])
        w = jnp.zeros_like(act)
        for h in range(P_HEADS):
            p = e1_ref[h, j:j + 1, :] * e2_ref[h]
            w = w + jnp.where(p >= tau_ref[h:h + 1, :], p, 0.0)
        wa_ref[j * P_NKEYS:(j + 1) * P_NKEYS, :] = (w * act).astype(BF16)
    acc_ref[...] += _dot(vt_ref[...], wa_ref[...])

    @pl.when(e == ne - 1)
    def _():
        o_ref[...] = _layer_norm(DN_ALPHA * t_ref[...] + acc_ref[...].T, g_ref[...], b_ref[...])


def peer_experts(t, u, vt, e1, e2, tau, g, b, tm=512, rows=8):
    n, d = t.shape
    tm = min(tm, n)
    te = rows * P_NKEYS
    ne = P_EXPERTS // te
    return pl.pallas_call(
        functools.partial(_peer_experts_kernel, ne=ne, rows=rows),
        grid=(n // tm, ne),
        in_specs=[pl.BlockSpec((tm, d), lambda i, e: (i, 0)),
                  pl.BlockSpec((te, d), lambda i, e: (e, 0)),
                  pl.BlockSpec((d, te), lambda i, e: (0, e)),
                  pl.BlockSpec((P_HEADS, rows, tm), lambda i, e: (0, e, i)),
                  pl.BlockSpec((P_HEADS, P_NKEYS, tm), lambda i, e: (0, 0, i)),
                  pl.BlockSpec((P_HEADS, tm), lambda i, e: (0, i)),
                  pl.BlockSpec((1, d), lambda i, e: (0, 0)),
                  pl.BlockSpec((1, d), lambda i, e: (0, 0))],
        out_specs=pl.BlockSpec((tm, d), lambda i, e: (i, 0)),
        out_shape=jax.ShapeDtypeStruct((n, d), F32),
        scratch_shapes=[pltpu.VMEM((tm, d), BF16), pltpu.VMEM((d, tm), F32), pltpu.VMEM((te, tm), BF16)],
        compiler_params=_params("arbitrary", "arbitrary"),
        name="peer_experts",
    )(t, u, vt, e1, e2, tau, g.reshape(1, d), b.reshape(1, d))


def peer_layer(h, wq, sub_keys, u, vt, g, b):
    q = linear(h, wq)
    e1, e2, tau = peer_router(q, sub_keys)
    return peer_experts(h, u, vt, e1, e2, tau, g, b)


_GROUP_W = M_D_INNER // M_GROUPS
_HEADS_PER_GROUP = M_HEADS // M_GROUPS
_CONV_COLS = 512


def _silu(x):
    return x * (1.0 / (1.0 + jnp.exp(-x)))


def _softplus(x):
    return jnp.maximum(x, 0.0) + jnp.log(1.0 + jnp.exp(-jnp.abs(x)))


def _gate_norm(y, xs, z, d_skip, norm_g):
    y = (y + d_skip * xs) * _silu(z)
    return y * lax.rsqrt(jnp.mean(y * y, axis=-1, keepdims=True) + LN_EPS) * norm_g


def _pad_transpose(x):
    return jnp.concatenate([x, jnp.zeros_like(x)], axis=1).T


def _ssd_kernel(zx_ref, cw_ref, cb_ref, dtb_ref, alog_ref, dsk_ref, ng_ref,
                y_ref, st_ref, cv_ref, ubuf, xbc, state, *, nc):
    c = pl.program_id(1)
    q = M_CHUNK

    @pl.when(c == 0)
    def _():
        ubuf[0:8, :] = jnp.zeros((8, M_CONV_CH), F32)
        state[...] = jnp.zeros_like(state)

    ubuf[8:8 + q, :] = zx_ref[0, :, M_D_INNER:M_D_INNER + M_CONV_CH]
    for cc in range(M_CONV_CH // _CONV_COLS):
        lo, hi = cc * _CONV_COLS, (cc + 1) * _CONV_COLS
        acc = jnp.broadcast_to(cb_ref[:, lo:hi], (q, _CONV_COLS))
        for j in range(M_CONV):
            acc = acc + cw_ref[j:j + 1, lo:hi] * ubuf[5 + j:5 + j + q, lo:hi]
        xbc[:, lo:hi] = _silu(acc)
    tail = ubuf[q + 5:q + 8, :]
    ubuf[5:8, :] = tail
    cv_ref[0] = tail

    dt = _softplus(zx_ref[0, :, M_D_INNER + M_CONV_CH:M_IN] + dtb_ref[...])
    la = dt * (-jnp.exp(alog_ref[...]))
    tril = (_iota((q, q), 0) >= _iota((q, q), 1))
    acum = _dot(tril.astype(F32), la, precision=HIGHEST)
    exp_acum = jnp.exp(acum)
    dte = dt * jnp.exp(acum[q - 1:q, :] - acum)
    acum_t = _pad_transpose(acum)
    dt_t = _pad_transpose(dt)

    for g in range(M_GROUPS):
        b_g = xbc[:, M_D_INNER + g * M_STATE:M_D_INNER + (g + 1) * M_STATE]
        c_g = xbc[:, M_D_INNER + M_GROUPS * M_STATE + g * M_STATE:M_D_INNER + M_GROUPS * M_STATE + (g + 1) * M_STATE]
        b_gb = b_g.astype(BF16)
        cb = _dot_nt(c_g.astype(BF16), b_gb)
        ys = []
        for hl in range(_HEADS_PER_GROUP):
            h = g * _HEADS_PER_GROUP + hl
            xs_h = xbc[:, h * M_HEADDIM:(h + 1) * M_HEADDIM]
            col = jnp.broadcast_to(acum[:, h:h + 1], (q, q))
            row = jnp.broadcast_to(acum_t[h:h + 1, :], (q, q))
            decay = jnp.exp(jnp.where(tril, col - row, NEG))
            lmat = cb * decay * jnp.broadcast_to(dt_t[h:h + 1, :], (q, q))
            c_off = c_g * jnp.broadcast_to(exp_acum[:, h:h + 1], (q, M_STATE))
            s_in = state[h]
            y_h = _dot(lmat.astype(BF16), xs_h.astype(BF16)) + _dot_nt(c_off.astype(BF16), s_in.astype(BF16))
            ys.append(y_h)
            xw = xs_h * jnp.broadcast_to(dte[:, h:h + 1], (q, M_HEADDIM))
            upd = lax.dot_general(xw.astype(BF16), b_gb, (((0,), (0,)), ((), ())), preferred_element_type=F32)
            state[h] = s_in * jnp.broadcast_to(exp_acum[q - 1:q, h:h + 1], (M_HEADDIM, M_STATE)) + upd
        lo, hi = g * _GROUP_W, (g + 1) * _GROUP_W
        y_ref[0, :, lo:hi] = _gate_norm(jnp.concatenate(ys, axis=1), xbc[:, lo:hi], zx_ref[0, :, lo:hi],
                                        dsk_ref[:, lo:hi], ng_ref[:, lo:hi])

    @pl.when(c == nc - 1)
    def _():
        st_ref[0] = state[...]


def mamba_ssd(zx, conv_w, conv_b, dt_bias, a_log, d_skip, norm_g):
    b, l, _ = zx.shape
    nc = l // M_CHUNK
    full = lambda shape: pl.BlockSpec(shape, lambda i, c: (0,) * len(shape))
    return pl.pallas_call(
        functools.partial(_ssd_kernel, nc=nc),
        grid=(b, nc),
        in_specs=[pl.BlockSpec((1, M_CHUNK, M_IN), lambda i, c: (i, c, 0)),
                  full((M_CONV, M_CONV_CH)), full((1, M_CONV_CH)), full((1, M_HEADS)), full((1, M_HEADS)),
                  full((1, M_D_INNER)), full((1, M_D_INNER))],
        out_specs=[pl.BlockSpec((1, M_CHUNK, M_D_INNER), lambda i, c: (i, c, 0)),
                   pl.BlockSpec((1, M_HEADS, M_HEADDIM, M_STATE), lambda i, c: (i, 0, 0, 0)),
                   pl.BlockSpec((1, M_CONV - 1, M_CONV_CH), lambda i, c: (i, 0, 0))],
        out_shape=[jax.ShapeDtypeStruct((b, l, M_D_INNER), F32),
                   jax.ShapeDtypeStruct((b, M_HEADS, M_HEADDIM, M_STATE), F32),
                   jax.ShapeDtypeStruct((b, M_CONV - 1, M_CONV_CH), F32)],
        scratch_shapes=[pltpu.VMEM((M_CHUNK + 8, M_CONV_CH), F32), pltpu.VMEM((M_CHUNK, M_CONV_CH), F32),
                        pltpu.VMEM((M_HEADS, M_HEADDIM, M_STATE), F32)],
        compiler_params=_params("arbitrary", "arbitrary"),
        name="mamba_ssd",
    )(zx, conv_w, conv_b.reshape(1, -1), dt_bias.reshape(1, -1), a_log.reshape(1, -1),
      jnp.repeat(d_skip, M_HEADDIM).reshape(1, -1), norm_g.reshape(1, -1))


def _mamba_step_pre_kernel(zx_ref, cs_ref, cw_ref, cb_ref, dtb_ref, alog_ref,
                           xbc_ref, xdt_ref, dec_ref, nb_ref):
    x_new = zx_ref[:, M_D_INNER:M_D_INNER + M_CONV_CH]
    acc = cb_ref[...] + cw_ref[M_CONV - 1:M_CONV, :] * x_new
    for j in range(M_CONV - 1):
        acc = acc + cw_ref[j:j + 1, :] * cs_ref[j]
    xbc = _silu(acc)
    xbc_ref[...] = xbc
    for j in range(M_CONV - 2):
        nb_ref[j] = cs_ref[j + 1]
    nb_ref[M_CONV - 2] = x_new
    dt = _softplus(zx_ref[:, M_D_INNER + M_CONV_CH:M_IN] + dtb_ref[...])
    dec = jnp.exp(dt * (-jnp.exp(alog_ref[...])))
    expand = (_iota((M_HEADS, M_D_INNER), 1) // M_HEADDIM == _iota((M_HEADS, M_D_INNER), 0)).astype(F32)
    xdt_ref[...] = xbc[:, :M_D_INNER] * _dot(dt, expand, precision=HIGHEST)
    dec_ref[...] = _dot(dec, expand, precision=HIGHEST)


def _mamba_step_state_kernel(s0_ref, dec_ref, xdt_ref, b_ref, c_ref, st_ref, y_ref):
    for g in range(M_GROUPS):
        lo, hi = g * _GROUP_W, (g + 1) * _GROUP_W
        s = (s0_ref[0, lo:hi, :] * jnp.broadcast_to(dec_ref[0, lo:hi, :], (_GROUP_W, M_STATE))
             + jnp.broadcast_to(xdt_ref[0, lo:hi, :], (_GROUP_W, M_STATE)) * b_ref[0, g:g + 1, :])
        st_ref[0, lo:hi, :] = s
        y_ref[0, lo:hi, :] = jnp.sum(s * c_ref[0, g:g + 1, :], axis=-1, keepdims=True)


def _mamba_step_post_kernel(y_ref, xbc_ref, zx_ref, dsk_ref, ng_ref, o_ref):
    for g in range(M_GROUPS):
        lo, hi = g * _GROUP_W, (g + 1) * _GROUP_W
        o_ref[:, lo:hi] = _gate_norm(y_ref[:, lo:hi], xbc_ref[:, lo:hi], zx_ref[:, lo:hi],
                                     dsk_ref[:, lo:hi], ng_ref[:, lo:hi])


def mamba_step(zx, conv_state, ssm_state, conv_w, conv_b, dt_bias, a_log, d_skip, norm_g):
    b = zx.shape[0]
    xbc, xdt, dec, new_buf = pl.pallas_call(
        _mamba_step_pre_kernel,
        out_shape=[jax.ShapeDtypeStruct((b, M_CONV_CH), F32), jax.ShapeDtypeStruct((b, M_D_INNER), F32),
                   jax.ShapeDtypeStruct((b, M_D_INNER), F32), jax.ShapeDtypeStruct((M_CONV - 1, b, M_CONV_CH), F32)],
        compiler_params=pltpu.CompilerParams(vmem_limit_bytes=VMEM_LIMIT_BYTES),
        name="mamba_step_pre",
    )(zx, jnp.swapaxes(conv_state, 0, 1), conv_w, conv_b.reshape(1, -1), dt_bias.reshape(1, -1), a_log.reshape(1, -1))
    hp = M_HEADS * M_HEADDIM
    col = lambda: pl.BlockSpec((1, hp, 1), lambda i: (i, 0, 0))
    grp = lambda: pl.BlockSpec((1, M_GROUPS, M_STATE), lambda i: (i, 0, 0))
    st, y = pl.pallas_call(
        _mamba_step_state_kernel,
        grid=(b,),
        in_specs=[pl.BlockSpec((1, hp, M_STATE), lambda i: (i, 0, 0)), col(), col(), grp(), grp()],
        out_specs=[pl.BlockSpec((1, hp, M_STATE), lambda i: (i, 0, 0)), col()],
        out_shape=[jax.ShapeDtypeStruct((b, hp, M_STATE), F32), jax.ShapeDtypeStruct((b, hp, 1), F32)],
        compiler_params=_params("arbitrary"),
        name="mamba_step_state",
    )(ssm_state.reshape(b, hp, M_STATE), dec.reshape(b, hp, 1), xdt.reshape(b, hp, 1),
      xbc[:, M_D_INNER:M_D_INNER + M_GROUPS * M_STATE].reshape(b, M_GROUPS, M_STATE),
      xbc[:, M_D_INNER + M_GROUPS * M_STATE:].reshape(b, M_GROUPS, M_STATE))
    y = pl.pallas_call(
        _mamba_step_post_kernel,
        out_shape=jax.ShapeDtypeStruct((b, M_D_INNER), F32),
        compiler_params=pltpu.CompilerParams(vmem_limit_bytes=VMEM_LIMIT_BYTES),
        name="mamba_step_post",
    )(y.reshape(b, hp), xbc, zx, jnp.repeat(d_skip, M_HEADDIM).reshape(1, -1), norm_g.reshape(1, -1))
    return y, jnp.swapaxes(new_buf, 0, 1), st.reshape(b, M_HEADS, M_HEADDIM, M_STATE)


def mamba_layer(hp, hs, conv_state, ssm_state, w_in, conv_w, conv_b, dt_bias, a_log, d_skip, norm_g, w_out, g, b, batch):
    n, d = hp.shape
    mw = (conv_w, conv_b, dt_bias, a_log, d_skip, norm_g)
    y_p, st_p, cv_p = mamba_ssd(linear(hp, w_in).reshape(batch, n // batch, M_IN), *mw)
    out_p = linear_res_ln(y_p.reshape(n, M_D_INNER), w_out, hp, g, b)
    y_s, cv_s, st_s = mamba_step(linear(hs, w_in), conv_state, ssm_state, *mw)
    out_s = linear_res_ln(y_s, w_out, hs, g, b)
    return out_p, out_s, st_p, cv_p, st_s, cv_s


_F_Q = N_HEADS * HEAD_DIM
_F_KV = N_KV * HEAD_DIM
F_IN = _F_Q + 2 * _F_KV + N_HEADS
_KV_BLOCK = 128


def _log_sigmoid(x):
    return -_softplus(-x)


def _eye(n):
    return (_iota((n, n), 0) == _iota((n, n), 1)).astype(F32)


def _transpose_small(x):
    return _dot_nt(_eye(x.shape[1]), x, precision=HIGHEST)


def _fox_logf_kernel(p_ref, bf_ref, lf_ref, c_ref, ct_ref, carry, *, tq):
    @pl.when(pl.program_id(1) == 0)
    def _():
        carry[...] = jnp.zeros_like(carry)

    lf = _log_sigmoid(p_ref[0, :, :N_HEADS] + bf_ref[...])
    tril = (_iota((tq, tq), 0) >= _iota((tq, tq), 1)).astype(F32)
    c = _dot(tril, lf, precision=HIGHEST) + carry[...]
    carry[...] = c[tq - 1:tq, :]
    lf_ref[0] = lf
    c_ref[0] = c
    ct_ref[0] = _transpose_small(c)


def fox_logf(proj, b_f, tq=256):
    b, t, _ = proj.shape
    tq = min(tq, t)
    last = (F_IN - N_HEADS) // LANES
    return pl.pallas_call(
        functools.partial(_fox_logf_kernel, tq=tq),
        grid=(b, t // tq),
        in_specs=[pl.BlockSpec((1, tq, LANES), lambda i, j: (i, j, last)),
                  pl.BlockSpec((1, N_HEADS), lambda i, j: (0, 0))],
        out_specs=[pl.BlockSpec((1, tq, N_HEADS), lambda i, j: (i, j, 0)),
                   pl.BlockSpec((1, tq, N_HEADS), lambda i, j: (i, j, 0)),
                   pl.BlockSpec((1, N_HEADS, tq), lambda i, j: (i, 0, j))],
        out_shape=[jax.ShapeDtypeStruct((b, t, N_HEADS), F32), jax.ShapeDtypeStruct((b, t, N_HEADS), F32),
                   jax.ShapeDtypeStruct((b, N_HEADS, t), F32)],
        scratch_shapes=[pltpu.VMEM((1, N_HEADS), F32)],
        compiler_params=_params("arbitrary", "arbitrary"),
        name="fox_logf",
    )(proj, b_f.reshape(1, N_HEADS))


def _flash_blocks(q, k_ref, v_ref, lo, hi, score_fn):
    tq, d = q.shape

    def body(j, carry):
        m, l, acc = carry
        start = pl.multiple_of(j * _KV_BLOCK, _KV_BLOCK)
        k = k_ref[0, pl.ds(start, _KV_BLOCK), :].astype(BF16)
        v = v_ref[0, pl.ds(start, _KV_BLOCK), :].astype(BF16)
        s = score_fn(j, _dot_nt(q, k) * ATT_SCALE)
        m_new = jnp.maximum(m, jnp.max(s, axis=-1, keepdims=True))
        a = jnp.exp(m - m_new)
        p = jnp.exp(s - m_new)
        return m_new, a * l + jnp.sum(p, axis=-1, keepdims=True), a * acc + _dot(p.astype(BF16), v)

    init = (jnp.full((tq, 1), NEG, F32), jnp.zeros((tq, 1), F32), jnp.zeros((tq, d), F32))
    m, l, acc = lax.fori_loop(lo, hi, body, init)
    return acc * (1.0 / l)


def _fox_attn_kernel(q_ref, k_ref, v_ref, c_ref, ct_ref, o_ref, *, tq):
    qi = pl.program_id(2)
    qpos = qi * tq + _iota((tq, _KV_BLOCK), 0)
    for r in range(REP):
        cq = c_ref[0, 0, :, r:r + 1]

        def score(j, s, r=r, cq=cq):
            kpos = j * _KV_BLOCK + _iota((tq, _KV_BLOCK), 1)
            return jnp.where(kpos <= qpos, s + cq - ct_ref[0, 0, j, r:r + 1, :], NEG)

        q = q_ref[0, :, r * HEAD_DIM:(r + 1) * HEAD_DIM].astype(BF16)
        o_ref[0, :, r * HEAD_DIM:(r + 1) * HEAD_DIM] = _flash_blocks(
            q, k_ref, v_ref, 0, (qi * tq + tq) // _KV_BLOCK, score)


def fox_attention(proj, c, ct, tq=128):
    b, t, _ = proj.shape
    tq = min(tq, t)
    c4 = c.reshape(b, t, N_KV, REP).transpose(0, 2, 1, 3)
    ct4 = ct.reshape(b, N_KV, REP, t // _KV_BLOCK, _KV_BLOCK).transpose(0, 1, 3, 2, 4)
    rep_w = REP * HEAD_DIM
    return pl.pallas_call(
        functools.partial(_fox_attn_kernel, tq=tq),
        grid=(b, N_KV, t // tq),
        in_specs=[pl.BlockSpec((1, tq, rep_w), lambda i, g, j: (i, j, g)),
                  pl.BlockSpec((1, t, HEAD_DIM), lambda i, g, j: (i, 0, _F_Q // HEAD_DIM + g)),
                  pl.BlockSpec((1, t, HEAD_DIM), lambda i, g, j: (i, 0, (_F_Q + _F_KV) // HEAD_DIM + g)),
                  pl.BlockSpec((1, 1, tq, REP), lambda i, g, j: (i, g, j, 0)),
                  pl.BlockSpec((1, 1, t // _KV_BLOCK, REP, _KV_BLOCK), lambda i, g, j: (i, g, 0, 0, 0))],
        out_specs=pl.BlockSpec((1, tq, rep_w), lambda i, g, j: (i, j, g)),
        out_shape=jax.ShapeDtypeStruct((b, t, _F_Q), F32),
        compiler_params=_params("arbitrary", "arbitrary", "arbitrary"),
        name="fox_attention",
    )(proj, proj, proj, c4, ct4)


def _logf_step_kernel(p_ref, bf_ref, o_ref):
    o_ref[...] = _log_sigmoid(p_ref[...] + bf_ref[...])


def _group_rows(parts):
    grp = _iota(parts[0].shape, 0) // REP
    out = parts[-1]
    for g in range(len(parts) - 2, -1, -1):
        out = jnp.where(grp == g, parts[g], out)
    return out


def _fox_decode_kernel(pt_ref, q_ref, kn_ref, vn_ref, lfn_ref, kv_ref, lf_ref, o_ref, m_s, l_s, acc_s, suf_s, *, npages):
    j = pl.program_id(1)
    q = q_ref[0]

    @pl.when(j == 0)
    def _():
        m_s[...] = jnp.sum(q * kn_ref[0], axis=-1, keepdims=True) * ATT_SCALE
        l_s[...] = jnp.ones_like(l_s)
        acc_s[...] = vn_ref[0]
        suf_s[...] = lfn_ref[0]

    lf_t = _transpose_small(lf_ref[0])
    after = (_iota((_KV_BLOCK, _KV_BLOCK), 0) > _iota((_KV_BLOCK, _KV_BLOCK), 1)).astype(F32)
    bias = _dot(lf_t, after, precision=HIGHEST) + suf_s[...]
    qb = q.astype(BF16)
    s = _group_rows([_dot_nt(qb, kv_ref[0, :, g * HEAD_DIM:(g + 1) * HEAD_DIM].astype(BF16)) for g in range(N_KV)])
    s = s * ATT_SCALE + bias
    m_new = jnp.maximum(m_s[...], jnp.max(s, axis=-1, keepdims=True))
    a = jnp.exp(m_s[...] - m_new)
    p = jnp.exp(s - m_new)
    pb = p.astype(BF16)
    pv = _group_rows([_dot(pb, kv_ref[0, :, _F_KV + g * HEAD_DIM:_F_KV + (g + 1) * HEAD_DIM].astype(BF16))
                      for g in range(N_KV)])
    l_s[...] = a * l_s[...] + jnp.sum(p, axis=-1, keepdims=True)
    acc_s[...] = a * acc_s[...] + pv
    m_s[...] = m_new
    suf_s[...] = suf_s[...] + jnp.sum(lf_t, axis=-1, keepdims=True)

    @pl.when(j == npages - 1)
    def _():
        o_ref[0] = acc_s[...] * (1.0 / l_s[...])


def fox_decode(proj, b_f, cache_kv, cache_logf, page_table):
    bs = proj.shape[0]
    npages = page_table.shape[1]
    pool, page = cache_kv.shape[0], cache_kv.shape[1]
    logf = pl.pallas_call(
        _logf_step_kernel,
        out_shape=jax.ShapeDtypeStruct((bs, N_HEADS), F32),
        name="fox_logf_step",
    )(proj[:, _F_Q + 2 * _F_KV:], b_f.reshape(1, N_HEADS))
    heads = lambda x: jnp.repeat(x.reshape(bs, N_KV, HEAD_DIM), REP, axis=1)
    row = lambda: pl.BlockSpec((1, N_HEADS, HEAD_DIM), lambda i, j, pt: (i, 0, 0))
    paged = lambda w: pl.BlockSpec((1, page, w), lambda i, j, pt: (pt[i, npages - 1 - j], 0, 0))
    o = pl.pallas_call(
        functools.partial(_fox_decode_kernel, npages=npages),
        grid_spec=pltpu.PrefetchScalarGridSpec(
            num_scalar_prefetch=1,
            grid=(bs, npages),
            in_specs=[row(), row(), row(), pl.BlockSpec((1, N_HEADS, 1), lambda i, j, pt: (i, 0, 0)),
                      paged(2 * _F_KV), paged(N_HEADS)],
            out_specs=row(),
            scratch_shapes=[pltpu.VMEM((N_HEADS, 1), F32), pltpu.VMEM((N_HEADS, 1), F32),
                            pltpu.VMEM((N_HEADS, HEAD_DIM), F32), pltpu.VMEM((N_HEADS, 1), F32)]),
        out_shape=jax.ShapeDtypeStruct((bs, N_HEADS, HEAD_DIM), F32),
        compiler_params=_params("arbitrary", "arbitrary"),
        name="fox_decode",
    )(page_table, proj[:, :_F_Q].reshape(bs, N_HEADS, HEAD_DIM), heads(proj[:, _F_Q:_F_Q + _F_KV]),
      heads(proj[:, _F_Q + _F_KV:_F_Q + 2 * _F_KV]), logf.reshape(bs, N_HEADS, 1),
      cache_kv.reshape(pool, page, 2 * _F_KV), cache_logf)
    return o.reshape(bs, _F_Q), logf


def fox_layer(hp, hs, cache_kv, cache_logf, page_table, w_in, b_f, w_out, g, b, batch):
    n, d = hp.shape
    t = n // batch
    proj_p = linear(hp, w_in).reshape(batch, t, F_IN)
    lf_p, c, ct = fox_logf(proj_p, b_f)
    o_p = fox_attention(proj_p, c, ct).reshape(n, _F_Q)
    out_p = linear_res_ln(o_p, w_out, hp, g, b)
    kv_p = proj_p[:, :, _F_Q:_F_Q + 2 * _F_KV].reshape(batch, t, 2, N_KV, HEAD_DIM)
    proj_s = linear(hs, w_in)
    o_s, lf_s = fox_decode(proj_s, b_f, cache_kv, cache_logf, page_table)
    out_s = linear_res_ln(o_s, w_out, hs, g, b)
    bs = hs.shape[0]
    kv_s = proj_s[:, _F_Q:_F_Q + 2 * _F_KV].reshape(bs, 1, 2, N_KV, HEAD_DIM)
    return out_p, out_s, kv_p, lf_p, kv_s, lf_s.reshape(bs, 1, N_HEADS)


_N_Q = N_HEADS * HEAD_DIM
_N_KVW = N_KV * HEAD_DIM
N_IN = _N_Q + N_BRANCH_KV * _N_KVW + 3 * N_HEADS
_SEL_PER_KV = _KV_BLOCK // SEL_BLOCK
_ROWS = 8


def _branch_col(n, g):
    return (_N_Q + n * _N_KVW) // HEAD_DIM + g


def _overlap(ncmp, nsel):
    cstart = _iota((ncmp, nsel), 0) * CMP_STRIDE
    sstart = _iota((ncmp, nsel), 1) * SEL_BLOCK
    return ((cstart < sstart + SEL_BLOCK) & (cstart + CMP_LEN > sstart)).astype(F32)


def _nsa_compress_kernel(x_ref, wc_ref, pw_ref, o_ref):
    pooled = _dot(wc_ref[0, 0], x_ref[0], precision=HIGHEST)
    o_ref[0, 0, 0] = _dot(pooled.astype(BF16), pw_ref[0, 0].astype(BF16))


def _pool_matrix(wpos, ncmp, t):
    off = jnp.arange(t)[None, :] - CMP_STRIDE * jnp.arange(ncmp)[:, None]
    band = (off >= 0) & (off < CMP_LEN) & (jnp.arange(ncmp)[:, None] < ncmp - 1)
    w = wpos[:, jnp.clip(off, 0, CMP_LEN - 1), :]
    return jnp.where(band[None, :, :, None], w, 0.0).transpose(0, 3, 1, 2)


def nsa_compress(proj, wpos, cmp_proj):
    b, t, _ = proj.shape
    ncmp = t // CMP_STRIDE
    return pl.pallas_call(
        _nsa_compress_kernel,
        grid=(b, 2, N_KV),
        in_specs=[pl.BlockSpec((1, t, HEAD_DIM), lambda i, c, g: (i, 0, _branch_col(c, g))),
                  pl.BlockSpec((1, 1, ncmp, t), lambda i, c, g: (c, g, 0, 0)),
                  pl.BlockSpec((1, 1, HEAD_DIM, HEAD_DIM), lambda i, c, g: (c, g, 0, 0))],
        out_specs=pl.BlockSpec((1, 1, 1, ncmp, HEAD_DIM), lambda i, c, g: (i, c, g, 0, 0)),
        out_shape=jax.ShapeDtypeStruct((b, 2, N_KV, ncmp, HEAD_DIM), F32),
        compiler_params=_params("arbitrary", "arbitrary", "arbitrary"),
        name="nsa_compress",
    )(proj, _pool_matrix(wpos, ncmp, t), cmp_proj)


def _rank_select(imp, valid, nsel, col_fn):
    m_idx = _iota(imp.shape, imp.ndim - 1)
    rank = jnp.zeros(imp.shape, F32)
    for m2 in range(nsel):
        other = col_fn(m2)
        ahead = (other > imp) | ((other == imp) & (m2 < m_idx))
        rank = rank + jnp.where(ahead, 1.0, 0.0)
    return jnp.where((rank < SEL_TOPK) & valid, 1.0, 0.0)


def _nsa_attn_kernel(q_ref, kc_ref, vc_ref, ks_ref, vs_ref, kw_ref, vw_ref, gp_ref, gb_ref, o_ref, selx, *, tq, t):
    qi = pl.program_id(2)
    ncmp = t // CMP_STRIDE
    nsel = t // SEL_BLOCK
    nkb = t // _KV_BLOCK
    qcol = qi * tq + _iota((tq, 1), 0)

    n_idx = _iota((tq, ncmp), 1)
    cmask = (n_idx * CMP_STRIDE + CMP_LEN - 1 <= qcol) & (n_idx < ncmp - 1)
    kc = kc_ref[0, 0, 0].astype(BF16)
    vc = vc_ref[0, 0, 0].astype(BF16)
    qs, o_cmp = [], []
    pcsum = jnp.zeros((tq, ncmp), F32)
    for r in range(REP):
        q = q_ref[0, :, r * HEAD_DIM:(r + 1) * HEAD_DIM].astype(BF16)
        qs.append(q)
        s = jnp.where(cmask, _dot_nt(q, kc) * ATT_SCALE, NEG)
        p = jnp.exp(s - jnp.max(s, axis=-1, keepdims=True))
        pc = jnp.where(cmask, p * (1.0 / jnp.sum(p, axis=-1, keepdims=True)), 0.0)
        pcsum = pcsum + pc
        o_cmp.append(_dot(pc.astype(BF16), vc))
    imp = _dot(pcsum, _overlap(ncmp, LANES), precision=HIGHEST)
    m_idx = _iota((tq, LANES), 1)
    qblk = qcol // SEL_BLOCK
    forced = (m_idx == 0) | (m_idx == qblk) | (m_idx == qblk - 1)
    valid = (m_idx <= qblk) & (m_idx < nsel)
    imp = jnp.where(valid, jnp.where(forced, imp + FORCE_BONUS, imp), NEG)
    sel = _rank_select(imp, valid, nsel, lambda m2: jnp.broadcast_to(imp[:, m2:m2 + 1], imp.shape))
    first = _iota((tq, _KV_BLOCK), 1) < SEL_BLOCK
    for j in range(nkb):
        a = jnp.broadcast_to(sel[:, _SEL_PER_KV * j:_SEL_PER_KV * j + 1], (tq, _KV_BLOCK))
        c = jnp.broadcast_to(sel[:, _SEL_PER_KV * j + 1:_SEL_PER_KV * j + 2], (tq, _KV_BLOCK))
        selx[j] = jnp.where(first, a, c)

    gates = 1.0 / (1.0 + jnp.exp(-(gp_ref[0, 0] + gb_ref[0])))
    qpos = qi * tq + _iota((tq, _KV_BLOCK), 0)

    def slc_score(j, s):
        kpos = j * _KV_BLOCK + _iota((tq, _KV_BLOCK), 1)
        return jnp.where((selx[j] > 0.5) & (kpos <= qpos), s, NEG)

    def win_score(j, s):
        kpos = j * _KV_BLOCK + _iota((tq, _KV_BLOCK), 1)
        return jnp.where((kpos <= qpos) & (kpos > qpos - WINDOW), s, NEG)

    hi = (qi * tq + tq) // _KV_BLOCK
    lo_win = jnp.maximum((qi * tq - WINDOW) // _KV_BLOCK, 0)
    for r in range(REP):
        o_slc = _flash_blocks(qs[r], ks_ref, vs_ref, 0, hi, slc_score)
        o_win = _flash_blocks(qs[r], kw_ref, vw_ref, lo_win, hi, win_score)
        gate = lambda br, r=r: jnp.broadcast_to(gates[:, 3 * r + br:3 * r + br + 1], (tq, HEAD_DIM))
        o_ref[0, :, r * HEAD_DIM:(r + 1) * HEAD_DIM] = gate(0) * o_cmp[r] + gate(1) * o_slc + gate(2) * o_win


def nsa_attention(proj, kvc, b_gate, tq=128):
    b, t, _ = proj.shape
    tq = min(tq, t)
    ncmp = t // CMP_STRIDE
    rep_w = REP * HEAD_DIM
    gate_pre = proj[:, :, _N_Q + N_BRANCH_KV * _N_KVW:].reshape(b, t, N_KV, 3 * REP).transpose(0, 2, 1, 3)
    col = lambda n: pl.BlockSpec((1, t, HEAD_DIM), lambda i, g, j, n=n: (i, 0, _branch_col(n, g)))
    cmp = lambda c: pl.BlockSpec((1, 1, 1, ncmp, HEAD_DIM), lambda i, g, j, c=c: (i, c, g, 0, 0))
    return pl.pallas_call(
        functools.partial(_nsa_attn_kernel, tq=tq, t=t),
        grid=(b, N_KV, t // tq),
        in_specs=[pl.BlockSpec((1, tq, rep_w), lambda i, g, j: (i, j, g)),
                  cmp(0), cmp(1), col(2), col(3), col(4), col(5),
                  pl.BlockSpec((1, 1, tq, 3 * REP), lambda i, g, j: (i, g, j, 0)),
                  pl.BlockSpec((1, 1, 3 * REP), lambda i, g, j: (g, 0, 0))],
        out_specs=pl.BlockSpec((1, tq, rep_w), lambda i, g, j: (i, j, g)),
        out_shape=jax.ShapeDtypeStruct((b, t, _N_Q), F32),
        scratch_shapes=[pltpu.VMEM((t // _KV_BLOCK, tq, _KV_BLOCK), F32)],
        compiler_params=_params("arbitrary", "arbitrary", "arbitrary"),
        name="nsa_attention",
    )(proj, kvc, kvc, proj, proj, proj, proj, gate_pre, b_gate.reshape(N_KV, 1, 3 * REP))


_STRIDES_PER_PAGE = 128 // CMP_STRIDE
_IDX_LANES = LANES


def _nsa_pool_pages_kernel(pt_ref, page_ref, wh_ref, wt_ref, head_ref, tail_ref):
    for c in range(2):
        for g in range(N_KV):
            lo = c * _N_KVW + g * HEAD_DIM
            x = page_ref[0, :, lo:lo + HEAD_DIM]
            head_ref[0, :, lo:lo + HEAD_DIM] = _dot(wh_ref[c, g], x, precision=HIGHEST)
            tail_ref[0, :, lo:lo + HEAD_DIM] = _dot(wt_ref[c, g], x, precision=HIGHEST)


def _stride_weights(wpos, first):
    page = _STRIDES_PER_PAGE * CMP_STRIDE
    off = jnp.arange(page)[None, :] - CMP_STRIDE * jnp.arange(_STRIDES_PER_PAGE)[:, None]
    band = (off >= 0) & (off < CMP_STRIDE)
    w = wpos[:, first + jnp.clip(off, 0, CMP_STRIDE - 1), :]
    return jnp.where(band[None, :, :, None], w, 0.0).transpose(0, 3, 1, 2)


def nsa_pool_pages(cache, page_table, wpos):
    bs, npages = page_table.shape
    pool, page = cache.shape[0], cache.shape[1]
    width = 2 * _N_KVW
    full = lambda: pl.BlockSpec((2, N_KV, _STRIDES_PER_PAGE, page), lambda i, j, pt: (0, 0, 0, 0))
    out = lambda: pl.BlockSpec((1, _STRIDES_PER_PAGE, width), lambda i, j, pt: (i, j, 0))
    shape = jax.ShapeDtypeStruct((bs, npages * _STRIDES_PER_PAGE, width), F32)
    return pl.pallas_call(
        _nsa_pool_pages_kernel,
        grid_spec=pltpu.PrefetchScalarGridSpec(
            num_scalar_prefetch=1,
            grid=(bs, npages),
            in_specs=[pl.BlockSpec((1, page, width), lambda i, j, pt: (pt[i, j], 0, 0)), full(), full()],
            out_specs=[out(), out()]),
        out_shape=[shape, shape],
        compiler_params=_params("arbitrary", "arbitrary"),
        name="nsa_pool_pages",
    )(page_table, cache.reshape(pool, page, -1), _stride_weights(wpos, 0), _stride_weights(wpos, CMP_STRIDE))


def _nsa_decode_cmp_kernel(q_ref, head_ref, tail_ref, pw_ref, ocmp_ref, idx_ref, *, past, nsp):
    nst = head_ref.shape[1]
    n_idx = _iota((_ROWS, nst), 1)
    cmask = n_idx * CMP_STRIDE + CMP_LEN - 1 <= past
    real = _iota((_ROWS, nst), 0) < REP
    overlap = _overlap(nst, nsp)
    m_idx = _iota((_ROWS, nsp), 1)
    qblk = past // SEL_BLOCK
    forced = (m_idx == 0) | (m_idx == qblk) | (m_idx == qblk - 1)
    valid = m_idx <= qblk
    sq_r = _iota((nsp, nsp), 0)
    sq_c = _iota((nsp, nsp), 1)
    for g in range(N_KV):
        lo = g * HEAD_DIM
        pooled_k = (head_ref[0, :, lo:lo + HEAD_DIM] + tail_ref[0, :, lo:lo + HEAD_DIM]).astype(BF16)
        pooled_v = (head_ref[0, :, _N_KVW + lo:_N_KVW + lo + HEAD_DIM]
                    + tail_ref[0, :, _N_KVW + lo:_N_KVW + lo + HEAD_DIM]).astype(BF16)
        kc = _dot(pooled_k, pw_ref[0, g].astype(BF16)).astype(BF16)
        vc = _dot(pooled_v, pw_ref[1, g].astype(BF16)).astype(BF16)
        s = jnp.where(cmask, _dot_nt(q_ref[0, g].astype(BF16), kc) * ATT_SCALE, NEG)
        p = jnp.exp(s - jnp.max(s, axis=-1, keepdims=True))
        pc = jnp.where(cmask, p * (1.0 / jnp.sum(p, axis=-1, keepdims=True)), 0.0)
        ocmp_ref[0, g] = _dot(pc.astype(BF16), vc)
        pcsum = jnp.sum(jnp.where(real, pc, 0.0), axis=0, keepdims=True)
        imp = _dot(jnp.broadcast_to(pcsum, (_ROWS, nst)), overlap, precision=HIGHEST)
        imp = jnp.where(valid, jnp.where(forced, imp + FORCE_BONUS, imp), NEG)
        other = jnp.broadcast_to(_transpose_small(imp)[:, 0:1], (nsp, nsp))
        mine = jnp.broadcast_to(imp[0:1, :], (nsp, nsp))
        ahead = (other > mine) | ((other == mine) & (sq_r < sq_c))
        rank = jnp.sum(jnp.where(ahead, 1.0, 0.0), axis=0, keepdims=True)
        sel = jnp.where((rank < SEL_TOPK) & valid[0:1, :], 1.0, 0.0)
        before = _dot(jnp.broadcast_to(sel, (_ROWS, nsp)), (sq_r < sq_c).astype(F32))[0:1, :]
        lane = _iota((1, _IDX_LANES), 1)
        row = jnp.where(lane == SEL_TOPK, jnp.sum(sel, axis=-1, keepdims=True), 0.0)
        for k in range(SEL_TOPK):
            hit = (sel > 0.5) & (before == k)
            blk = jnp.sum(jnp.where(hit, m_idx[0:1, :].astype(F32), 0.0), axis=-1, keepdims=True)
            row = jnp.where(lane == k, blk, row)
        idx_ref[0, g:g + 1, :] = row.astype(jnp.int32)


def nsa_decode_cmp(q8, head, tail_next, cmp_proj, past):
    bs, nst, width = head.shape
    nsp = -(-(past // SEL_BLOCK + 1) // LANES) * LANES
    grp = lambda: pl.BlockSpec((1, N_KV, _ROWS, HEAD_DIM), lambda i: (i, 0, 0, 0))
    return pl.pallas_call(
        functools.partial(_nsa_decode_cmp_kernel, past=past, nsp=nsp),
        grid=(bs,),
        in_specs=[grp(), pl.BlockSpec((1, nst, width), lambda i: (i, 0, 0)),
                  pl.BlockSpec((1, nst, width), lambda i: (i, 0, 0)),
                  pl.BlockSpec((2, N_KV, HEAD_DIM, HEAD_DIM), lambda i: (0, 0, 0, 0))],
        out_specs=[grp(), pl.BlockSpec((1, N_KV, _IDX_LANES), lambda i: (i, 0, 0))],
        out_shape=[jax.ShapeDtypeStruct((bs, N_KV, _ROWS, HEAD_DIM), F32),
                   jax.ShapeDtypeStruct((bs, N_KV, _IDX_LANES), jnp.int32)],
        compiler_params=_params("arbitrary"),
        name="nsa_decode_cmp",
    )(q8, head, tail_next, cmp_proj)


def _online_update(m_s, l_s, acc_s, s, v):
    m_new = jnp.maximum(m_s[...], jnp.max(s, axis=-1, keepdims=True))
    a = jnp.exp(m_s[...] - m_new)
    p = jnp.exp(s - m_new)
    l_s[...] = a * l_s[...] + jnp.sum(p, axis=-1, keepdims=True)
    acc_s[...] = a * acc_s[...] + _dot(p.astype(BF16), v.astype(BF16))
    m_s[...] = m_new


def _nsa_decode_slc_kernel(pt_ref, idx_ref, q_ref, kn_ref, vn_ref, kp_ref, vp_ref, o_ref, m_s, l_s, acc_s, *, npages):
    i, g, k = pl.program_id(0), pl.program_id(1), pl.program_id(2)
    blk = idx_ref[i, g, k]
    live = k < idx_ref[i, g, SEL_TOPK]
    new_blk = npages * _SEL_PER_KV
    q = q_ref[0, 0].astype(BF16)

    @pl.when(k == 0)
    def _():
        m_s[...] = jnp.full_like(m_s, NEG)
        l_s[...] = jnp.zeros_like(l_s)
        acc_s[...] = jnp.zeros_like(acc_s)

    @pl.when(live & (blk < new_blk))
    def _():
        start = pl.multiple_of((blk % _SEL_PER_KV) * SEL_BLOCK, SEL_BLOCK)
        keys = kp_ref[0, pl.ds(start, SEL_BLOCK), :].astype(BF16)
        _online_update(m_s, l_s, acc_s, _dot_nt(q, keys) * ATT_SCALE, vp_ref[0, pl.ds(start, SEL_BLOCK), :])

    @pl.when(live & (blk == new_blk))
    def _():
        s = _dot_nt(q, kn_ref[0, 0].astype(BF16)) * ATT_SCALE
        s = jnp.where(_iota(s.shape, 1) == 0, s, NEG)
        _online_update(m_s, l_s, acc_s, s, vn_ref[0, 0])

    @pl.when(k == SEL_TOPK - 1)
    def _():
        o_ref[0, 0] = acc_s[...] * (1.0 / l_s[...])


def nsa_decode_slc(q8, kn8, vn8, cache, page_table, idx):
    bs, npages = page_table.shape
    pool, page = cache.shape[0], cache.shape[1]
    grp = lambda: pl.BlockSpec((1, 1, _ROWS, HEAD_DIM), lambda i, g, k, pt, ix: (i, g, 0, 0))

    def paged(branch):
        def index(i, g, k, pt, ix):
            return pt[i, jnp.minimum(ix[i, g, k] // _SEL_PER_KV, npages - 1)], 0, branch * N_KV + g
        return pl.BlockSpec((1, page, HEAD_DIM), index)

    return pl.pallas_call(
        functools.partial(_nsa_decode_slc_kernel, npages=npages),
        grid_spec=pltpu.PrefetchScalarGridSpec(
            num_scalar_prefetch=2,
            grid=(bs, N_KV, SEL_TOPK),
            in_specs=[grp(), grp(), grp(), paged(2), paged(3)],
            out_specs=grp(),
            scratch_shapes=[pltpu.VMEM((_ROWS, 1), F32), pltpu.VMEM((_ROWS, 1), F32),
                            pltpu.VMEM((_ROWS, HEAD_DIM), F32)]),
        out_shape=jax.ShapeDtypeStruct((bs, N_KV, _ROWS, HEAD_DIM), F32),
        compiler_params=_params("arbitrary", "arbitrary", "arbitrary"),
        name="nsa_decode_slc",
    )(page_table, idx, q8, kn8, vn8, cache.reshape(pool, page, -1), cache.reshape(pool, page, -1))


def _nsa_decode_win_kernel(q_ref, kn_ref, vn_ref, kw_ref, vw_ref, ocmp_ref, oslc_ref, gp_ref, gb_ref, o_ref, *, past):
    q = q_ref[0, 0].astype(BF16)
    s_buf = _dot_nt(q, kw_ref[0].astype(BF16)) * ATT_SCALE
    kpos = past - WINDOW + _iota(s_buf.shape, 1)
    s_buf = jnp.where((kpos > past - WINDOW) & (kpos >= 0), s_buf, NEG)
    s_new = _dot_nt(q, kn_ref[0, 0].astype(BF16)) * ATT_SCALE
    s_new = jnp.where(_iota(s_new.shape, 1) == 0, s_new, NEG)
    m = jnp.maximum(jnp.max(s_buf, axis=-1, keepdims=True), jnp.max(s_new, axis=-1, keepdims=True))
    p_buf = jnp.exp(s_buf - m)
    p_new = jnp.exp(s_new - m)
    inv = 1.0 / (jnp.sum(p_buf, axis=-1, keepdims=True) + jnp.sum(p_new, axis=-1, keepdims=True))
    o_win = (_dot(p_buf.astype(BF16), vw_ref[0].astype(BF16)) + _dot(p_new.astype(BF16), vn_ref[0, 0].astype(BF16))) * inv
    gates = 1.0 / (1.0 + jnp.exp(-(gp_ref[0, 0] + gb_ref[0])))
    gate = lambda br: jnp.broadcast_to(gates[:, br:br + 1], (_ROWS, HEAD_DIM))
    o_ref[0, 0] = gate(0) * ocmp_ref[0, 0] + gate(1) * oslc_ref[0, 0] + gate(2) * o_win


def nsa_decode_win(q8, kn8, vn8, win, o_cmp, o_slc, gate_pre, b_gate, past):
    bs = q8.shape[0]
    grp = lambda: pl.BlockSpec((1, 1, _ROWS, HEAD_DIM), lambda i, g: (i, g, 0, 0))
    buf = lambda v: pl.BlockSpec((1, WINDOW, HEAD_DIM), lambda i, g, v=v: (i, 0, v * N_KV + g))
    return pl.pallas_call(
        functools.partial(_nsa_decode_win_kernel, past=past),
        grid=(bs, N_KV),
        in_specs=[grp(), grp(), grp(), buf(0), buf(1), grp(), grp(),
                  pl.BlockSpec((1, 1, _ROWS, 3), lambda i, g: (i, g, 0, 0)),
                  pl.BlockSpec((1, _ROWS, 3), lambda i, g: (g, 0, 0))],
        out_specs=grp(),
        out_shape=jax.ShapeDtypeStruct((bs, N_KV, _ROWS, HEAD_DIM), F32),
        compiler_params=_params("arbitrary", "arbitrary"),
        name="nsa_decode_win",
    )(q8, kn8, vn8, win.reshape(bs, WINDOW, -1), win.reshape(bs, WINDOW, -1), o_cmp, o_slc, gate_pre, b_gate)


def _pad_rows(x, rows):
    pad = [(0, 0)] * x.ndim
    pad[-2] = (0, rows - x.shape[-2])
    return jnp.pad(x, pad)


def nsa_decode(proj, cache, win, page_table, b_gate, wpos, cmp_proj):
    bs, npages = page_table.shape
    past = npages * cache.shape[1]
    q8 = _pad_rows(proj[:, :_N_Q].reshape(bs, N_KV, REP, HEAD_DIM), _ROWS)
    new = lambda n: _pad_rows(proj[:, _N_Q + n * _N_KVW:_N_Q + (n + 1) * _N_KVW].reshape(bs, N_KV, 1, HEAD_DIM), _ROWS)
    head, tail = nsa_pool_pages(cache, page_table, wpos)
    tail_next = jnp.concatenate([tail[:, 1:], jnp.zeros_like(tail[:, :1])], axis=1)
    o_cmp, idx = nsa_decode_cmp(q8, head, tail_next, cmp_proj, past)
    o_slc = nsa_decode_slc(q8, new(2), new(3), cache, page_table, idx)
    gate_pre = _pad_rows(proj[:, _N_Q + N_BRANCH_KV * _N_KVW:].reshape(bs, N_KV, REP, 3), _ROWS)
    o = nsa_decode_win(q8, new(4), new(5), win, o_cmp, o_slc, gate_pre,
                       _pad_rows(b_gate.reshape(N_KV, REP, 3), _ROWS), past)
    return o[:, :, :REP].reshape(bs, _N_Q)


def nsa_layer(hp, hs, cache, win, page_table, w_in, b_gate, wpos, cmp_proj, w_out, g, b, batch):
    n, d = hp.shape
    t = n // batch
    bs = hs.shape[0]
    proj_p = linear(hp, w_in).reshape(batch, t, N_IN)
    o_p = nsa_attention(proj_p, nsa_compress(proj_p, wpos, cmp_proj), b_gate).reshape(n, _N_Q)
    out_p = linear_res_ln(o_p, w_out, hp, g, b)
    kv_p = proj_p[:, :, _N_Q:_N_Q + 4 * _N_KVW].reshape(batch, t, 4, N_KV, HEAD_DIM)
    win_rows = proj_p[:, :, _N_Q + 4 * _N_KVW:_N_Q + 6 * _N_KVW].reshape(batch, t, 2, N_KV, HEAD_DIM)
    win_p = jnp.concatenate([jnp.zeros((batch, WINDOW, 2, N_KV, HEAD_DIM), F32), win_rows], axis=1)[:, t:]
    proj_s = linear(hs, w_in)
    o_s = nsa_decode(proj_s, cache, win, page_table, b_gate, wpos, cmp_proj)
    out_s = linear_res_ln(o_s, w_out, hs, g, b)
    kv_s = proj_s[:, _N_Q:_N_Q + 4 * _N_KVW].reshape(bs, 1, 4, N_KV, HEAD_DIM)
    win_new = proj_s[:, _N_Q + 4 * _N_KVW:_N_Q + 6 * _N_KVW].reshape(bs, 1, 2, N_KV, HEAD_DIM)
    win_s = jnp.concatenate([win, win_new], axis=1)[:, 1:]
    return out_p, out_s, kv_p, win_p, kv_s, win_s


def _mem_step(hs, mem_kv, wq, wo, g, b):
    bs, d = hs.shape
    q = jnp.broadcast_to(linear(hs, wq)[:, None, :], (bs, _ROWS, d))
    return linear_res_ln(mem_attention(q, mem_kv)[:, 0], wo, hs, g, b)


def _peer_step(hs, wq, sub_keys, u, vt, g, b):
    bs = hs.shape[0]
    return peer_layer(jnp.pad(hs, ((0, LANES - bs), (0, 0))), wq, sub_keys, u, vt, g, b)[:bs]


def kernel(x_prompt, x_sample, cache_mem_kv, state_ssm, state_conv, cache_fox_kv, cache_fox_logf, cache_nsa_kv,
           state_nsa_win, page_table, mem_prompt, ln_g, ln_b, mem_wq, mem_wkv, mem_wo, peer_wq, peer_subkeys,
           peer_u, peer_v, mamba_w_in, mamba_conv_w, mamba_conv_b, mamba_dt_bias, mamba_a_log, mamba_d,
           mamba_norm_g, mamba_w_out, fox_w_in, fox_b_f, fox_w_out, nsa_w_in, nsa_b_gate, nsa_cmp_wpos,
           nsa_cmp_proj, nsa_w_out):
    bp, t, d = x_prompt.shape
    bs = x_sample.shape[0]
    hp = x_prompt.reshape(bp * t, d)
    hs = x_sample.reshape(bs, d)
    bf = lambda w: w.astype(BF16)
    outs = {k: [] for k in ("mem_kv_p", "ssm_p", "conv_p", "fox_kv_p", "fox_lf_p", "nsa_kv_p", "nsa_win_p",
                            "ssm_s", "conv_s", "fox_kv_s", "fox_lf_s", "nsa_kv_s", "nsa_win_s")}
    for i in range(DEPTH):
        kind, j = i % 3, i // 3
        g, b = ln_g[i], ln_b[i]
        if kind == 0:
            hp, hs, st_p, cv_p, st_s, cv_s = mamba_layer(
                hp, hs, state_conv[j], state_ssm[j], bf(mamba_w_in[j]), mamba_conv_w[j], mamba_conv_b[j],
                mamba_dt_bias[j], mamba_a_log[j], mamba_d[j], mamba_norm_g[j], bf(mamba_w_out[j]), g[0], b[0], bp)
            outs["ssm_p"].append(st_p); outs["conv_p"].append(cv_p)
            outs["ssm_s"].append(st_s); outs["conv_s"].append(cv_s)
        elif kind == 1:
            hp, hs, kv_p, lf_p, kv_s, lf_s = fox_layer(
                hp, hs, cache_fox_kv[j], cache_fox_logf[j], page_table, bf(fox_w_in[j]), fox_b_f[j],
                bf(fox_w_out[j]), g[0], b[0], bp)
            outs["fox_kv_p"].append(kv_p); outs["fox_lf_p"].append(lf_p)
            outs["fox_kv_s"].append(kv_s); outs["fox_lf_s"].append(lf_s)
        else:
            hp, hs, kv_p, win_p, kv_s, win_s = nsa_layer(
                hp, hs, cache_nsa_kv[j], state_nsa_win[j], page_table, bf(nsa_w_in[j]), nsa_b_gate[j],
                nsa_cmp_wpos[j], nsa_cmp_proj[j], bf(nsa_w_out[j]), g[0], b[0], bp)
            outs["nsa_kv_p"].append(kv_p); outs["nsa_win_p"].append(win_p)
            outs["nsa_kv_s"].append(kv_s); outs["nsa_win_s"].append(win_s)
        wq, wo = bf(mem_wq[i]), bf(mem_wo[i])
        mkv = linear(mem_prompt.reshape(bp * MEM_LEN, d), bf(mem_wkv[i]))
        outs["mem_kv_p"].append(mkv.reshape(bp, MEM_LEN, 2, MEM_HEADS, MEM_HD))
        hp = mem_layer(hp, mkv.reshape(bp, MEM_LEN, 2 * d), wq, wo, g[1], b[1], bp)
        hs = _mem_step(hs, cache_mem_kv[i].reshape(bs, MEM_LEN, 2 * d), wq, wo, g[1], b[1])
        pw = (bf(peer_wq[i]), peer_subkeys[i], bf(peer_u[i]), bf(peer_v[i].T), g[2], b[2])
        hp = peer_layer(hp, *pw)
        hs = _peer_step(hs, *pw)
    stack = lambda k: jnp.stack(outs[k])
    return (hp.reshape(bp, t, d), hs.reshape(bs, 1, d), stack("mem_kv_p"), stack("ssm_p"), stack("conv_p"),
            stack("fox_kv_p"), stack("fox_lf_p"), stack("nsa_kv_p"), stack("nsa_win_p"), stack("ssm_s"),
            stack("conv_s"), stack("fox_kv_s"), stack("fox_lf_s"), stack("nsa_kv_s"), stack("nsa_win_s"))
```
